```python
import math
import jax, jax.numpy as jnp
from jax import lax
import numpy as np

D_MODEL = 2048
BATCH = 2
SEQ = 4096
DEPTH = 1
DEC_BATCH = 128
DEC_SEQ = 4
PAST_LEN = 2048
PAGE_SIZE = 128

SB_HEADS = 8
SB_HEAD_DIM = 128
SB_WIDTH = SB_HEADS * SB_HEAD_DIM
SB_BIAS_INIT = -7.0
Q_BLOCK = 128
CM_GROUPS = 8
CM_GROUP_DIM = 128
CM_WIDTH = CM_GROUPS * CM_GROUP_DIM
CHUNK = 128
MEM_LEN = 256
MEM_HEADS = 4
MEM_HEAD_DIM = 256
MEM_WIDTH = MEM_HEADS * MEM_HEAD_DIM
N_BRANCH = 3
IN_WIDTH = 3 * SB_WIDTH + 2 * CM_WIDTH + MEM_WIDTH + N_BRANCH * D_MODEL
D_FF = -(-8 * D_MODEL // (3 * 256)) * 256
EPS = 1e-6

kernel_name = "stickbreak_chunkmlp_memxattn_hybrid_step"


def rmsnorm(x, g):
    xf = x.astype(jnp.float32)
    y = xf * lax.rsqrt(jnp.mean(xf * xf, axis=-1, keepdims=True) + EPS)
    return (y * g.astype(jnp.float32)).astype(x.dtype)


def _sb_block(q, k, v, sb_bias, q_off):
    t = q.shape[1]
    l = k.shape[1]
    z = jnp.einsum("bthd,blhd->bhtl", q, k).astype(jnp.float32) * (1.0 / math.sqrt(SB_HEAD_DIM))
    z = z + sb_bias.astype(jnp.float32)[None, :, None, None]
    mask = jnp.arange(l)[None, :] < (q_off + jnp.arange(t))[:, None]
    log_beta = jax.nn.log_sigmoid(z)
    log_keep = jnp.where(mask, jax.nn.log_sigmoid(-z), 0.0)
    log_rest = lax.cumsum(log_keep, axis=3, reverse=True) - log_keep
    a = jnp.where(mask, jnp.exp(log_beta + log_rest), 0.0)
    o = jnp.einsum("bhtl,blhd->bthd", a, v.astype(jnp.float32))
    return o.astype(v.dtype)


def stick_breaking_attention(q, k, v, sb_bias, q_start):
    t = q.shape[1]
    outs = []
    for lo in range(0, t, Q_BLOCK):
        hi = min(lo + Q_BLOCK, t)
        kv_len = q_start + hi
        outs.append(_sb_block(q[:, lo:hi], k[:, :kv_len], v[:, :kv_len], sb_bias, q_start + lo))
    return jnp.concatenate(outs, axis=1)


def chunk_spatial_gate(u, v, w_s, b_s):
    b, l, _ = v.shape
    n_chunks = -(-l // CHUNK)
    pad = n_chunks * CHUNK - l
    vp = jnp.pad(v, ((0, 0), (0, pad), (0, 0))).reshape(b, n_chunks, CHUNK, CM_GROUPS, CM_GROUP_DIM)
    tri = jnp.tril(jnp.ones((CHUNK, CHUNK), dtype=bool))
    w = jnp.where(tri[None], w_s, 0.0)
    mixed = jnp.einsum("gts,bnsgc->bntgc", w, vp) + jnp.transpose(b_s)[:, :, None]
    mixed = mixed.reshape(b, n_chunks * CHUNK, CM_WIDTH)[:, :l]
    return u * mixed


def memory_kv(mem, g_mem, w_mem_k, w_mem_v):
    b, m, _ = mem.shape
    mn = rmsnorm(mem, g_mem)
    k = (mn @ w_mem_k).reshape(b, m, MEM_HEADS, MEM_HEAD_DIM)
    v = (mn @ w_mem_v).reshape(b, m, MEM_HEADS, MEM_HEAD_DIM)
    return k, v


def memory_attention(q, mem_k, mem_v):
    s = jnp.einsum("bthd,bmhd->bhtm", q, mem_k).astype(jnp.float32) * (1.0 / math.sqrt(MEM_HEAD_DIM))
    p = jax.nn.softmax(s, axis=-1)
    o = jnp.einsum("bhtm,bmhd->bthd", p, mem_v.astype(jnp.float32))
    return o.astype(q.dtype)


def gather_pages(pool, page_table):
    g = pool[page_table]
    return g.reshape(g.shape[0], g.shape[1] * g.shape[2], g.shape[3], g.shape[4])


def trunk_layer(x, k_past, v_past, mem_k, mem_v, q_start,
                g_mix, w_in, sb_bias, w_s, b_s, g_cm, w_sb_o, w_cm_o, w_mem_o, w_out,
                g_ffn, w_gate, w_up, w_down):
    b, l, _ = x.shape
    h = rmsnorm(x, g_mix)
    z = h @ w_in
    o1 = SB_WIDTH
    o2 = o1 + SB_WIDTH
    o3 = o2 + SB_WIDTH
    o4 = o3 + CM_WIDTH
    o5 = o4 + CM_WIDTH
    o6 = o5 + MEM_WIDTH
    q_sb, k_sb, v_sb, u_cm, v_cm, q_mem, gate_logits = jnp.split(z, [o1, o2, o3, o4, o5, o6], axis=-1)

    q_sb = q_sb.reshape(b, l, SB_HEADS, SB_HEAD_DIM)
    k_new = k_sb.reshape(b, l, SB_HEADS, SB_HEAD_DIM)
    v_new = v_sb.reshape(b, l, SB_HEADS, SB_HEAD_DIM)
    if k_past is None:
        k_all, v_all = k_new, v_new
    else:
        k_all = jnp.concatenate([k_past, k_new], axis=1)
        v_all = jnp.concatenate([v_past, v_new], axis=1)
    o_sb = stick_breaking_attention(q_sb, k_all, v_all, sb_bias, q_start).reshape(b, l, SB_WIDTH)

    u = jax.nn.gelu(u_cm)
    vc = rmsnorm(jax.nn.gelu(v_cm), g_cm)
    o_cm = chunk_spatial_gate(u, vc, w_s, b_s)

    o_mem = memory_attention(q_mem.reshape(b, l, MEM_HEADS, MEM_HEAD_DIM), mem_k, mem_v).reshape(b, l, MEM_WIDTH)

    gates = jax.nn.sigmoid(gate_logits).reshape(b, l, N_BRANCH, D_MODEL)
    merged = (gates[:, :, 0] * (o_sb @ w_sb_o)
              + gates[:, :, 1] * (o_cm @ w_cm_o)
              + gates[:, :, 2] * (o_mem @ w_mem_o))
    x = x + merged @ w_out

    hf = rmsnorm(x, g_ffn)
    x = x + (jax.nn.silu(hf @ w_gate) * (hf @ w_up)) @ w_down
    return x, k_new, v_new, vc.reshape(b, l, CM_GROUPS, CM_GROUP_DIM)


def setup_inputs(seed: int = 0) -> dict:
    key = jax.random.key(seed)
    ks = jax.random.split(key, 32)
    f32 = jnp.float32

    def nrm(k, shape, scale):
        return jax.random.normal(k, shape, f32) * scale

    def gain(k, shape):
        return 1.0 + 0.02 * jax.random.normal(k, shape, f32)

    n_pages = PAST_LEN // PAGE_SIZE
    n_used = DEC_BATCH * n_pages
    n_phys = n_used + -(-n_used // 4)
    perm = jax.random.permutation(ks[0], n_phys)
    page_table = perm[:n_used].reshape(DEC_BATCH, n_pages).astype(jnp.int32)

    return {
        "x_prompt": nrm(ks[1], (BATCH, SEQ, D_MODEL), 1.0),
        "x_sample": nrm(ks[2], (DEC_BATCH, DEC_SEQ, D_MODEL), 1.0),
        "cache_sb_k": nrm(ks[3], (DEPTH, n_phys, PAGE_SIZE, SB_HEADS, SB_HEAD_DIM), 1.0),
        "cache_sb_v": nrm(ks[4], (DEPTH, n_phys, PAGE_SIZE, SB_HEADS, SB_HEAD_DIM), 1.0),
        "cache_mem_k": nrm(ks[5], (DEPTH, DEC_BATCH, MEM_LEN, MEM_HEADS, MEM_HEAD_DIM), 1.0),
        "cache_mem_v": nrm(ks[6], (DEPTH, DEC_BATCH, MEM_LEN, MEM_HEADS, MEM_HEAD_DIM), 1.0),
        "page_table": page_table,
        "mem_prompt": nrm(ks[7], (BATCH, MEM_LEN, D_MODEL), 1.0),
        "g_mix": gain(ks[8], (DEPTH, D_MODEL)),
        "w_in": nrm(ks[9], (DEPTH, D_MODEL, IN_WIDTH), D_MODEL ** -0.5),
        "sb_bias": SB_BIAS_INIT + 0.1 * jax.random.normal(ks[25], (DEPTH, SB_HEADS), f32),
        "w_s": nrm(ks[10], (DEPTH, CM_GROUPS, CHUNK, CHUNK), CHUNK ** -0.5),
        "b_s": nrm(ks[11], (DEPTH, CM_GROUPS, CHUNK), 0.02),
        "g_cm": gain(ks[12], (DEPTH, CM_WIDTH)),
        "g_mem": gain(ks[13], (DEPTH, D_MODEL)),
        "w_mem_k": nrm(ks[14], (DEPTH, D_MODEL, MEM_WIDTH), D_MODEL ** -0.5),
        "w_mem_v": nrm(ks[15], (DEPTH, D_MODEL, MEM_WIDTH), D_MODEL ** -0.5),
        "w_sb_o": nrm(ks[16], (DEPTH, SB_WIDTH, D_MODEL), SB_WIDTH ** -0.5),
        "w_cm_o": nrm(ks[17], (DEPTH, CM_WIDTH, D_MODEL), CM_WIDTH ** -0.5),
        "w_mem_o": nrm(ks[18], (DEPTH, MEM_WIDTH, D_MODEL), MEM_WIDTH ** -0.5),
        "w_out": nrm(ks[19], (DEPTH, D_MODEL, D_MODEL), D_MODEL ** -0.5),
        "g_ffn": gain(ks[20], (DEPTH, D_MODEL)),
        "w_gate": nrm(ks[21], (DEPTH, D_MODEL, D_FF), D_MODEL ** -0.5),
        "w_up": nrm(ks[22], (DEPTH, D_MODEL, D_FF), D_MODEL ** -0.5),
        "w_down": nrm(ks[23], (DEPTH, D_FF, D_MODEL), D_FF ** -0.5),
        "g_final": gain(ks[24], (D_MODEL,)),
    }


def reference(x_prompt, x_sample, cache_sb_k, cache_sb_v, cache_mem_k, cache_mem_v, page_table,
              mem_prompt, g_mix, w_in, sb_bias, w_s, b_s, g_cm, g_mem, w_mem_k, w_mem_v, w_sb_o, w_cm_o,
              w_mem_o, w_out, g_ffn, w_gate, w_up, w_down, g_final):
    past_len = page_table.shape[1] * cache_sb_k.shape[2]
    xp, xs = x_prompt, x_sample
    sbk_p, sbv_p, mk_p, mv_p, sbk_s, sbv_s, cmv_s = [], [], [], [], [], [], []
    for layer in range(DEPTH):
        params = (g_mix[layer], w_in[layer], sb_bias[layer], w_s[layer], b_s[layer], g_cm[layer],
                  w_sb_o[layer], w_cm_o[layer], w_mem_o[layer], w_out[layer], g_ffn[layer],
                  w_gate[layer], w_up[layer], w_down[layer])
        mk, mv = memory_kv(mem_prompt, g_mem[layer], w_mem_k[layer], w_mem_v[layer])
        xp, kp, vp, _ = trunk_layer(xp, None, None, mk, mv, 0, *params)
        k_past = gather_pages(cache_sb_k[layer], page_table)
        v_past = gather_pages(cache_sb_v[layer], page_table)
        xs, ks_new, vs_new, cv = trunk_layer(xs, k_past, v_past, cache_mem_k[layer], cache_mem_v[layer],
                                             past_len, *params)
        sbk_p.append(kp)
        sbv_p.append(vp)
        mk_p.append(mk)
        mv_p.append(mv)
        sbk_s.append(ks_new)
        sbv_s.append(vs_new)
        cmv_s.append(cv)
    y_prompt = rmsnorm(xp, g_final)
    y_sample = rmsnorm(xs, g_final)
    return (y_prompt, y_sample, jnp.stack(sbk_p), jnp.stack(sbv_p), jnp.stack(mk_p), jnp.stack(mv_p),
            jnp.stack(sbk_s), jnp.stack(sbv_s), jnp.stack(cmv_s))
```

```python
import functools
import math

import jax
import jax.numpy as jnp
from jax import lax
from jax.experimental import pallas as pl
from jax.experimental.pallas import tpu as pltpu

F32 = jnp.float32
BF16 = jnp.bfloat16

EPS = 1e-6
V7X_LANES = 128
V7X_SUBLANES = 8
V7X_VMEM_BYTES = 64 << 20
VMEM_LIMIT = V7X_VMEM_BYTES - (8 << 20)

ROW_TILE = 512
SB_TILE = 128


def _params(*semantics):
    return pltpu.CompilerParams(dimension_semantics=semantics, vmem_limit_bytes=VMEM_LIMIT)


def _rmsnorm(x, g):
    return x * lax.rsqrt(jnp.mean(x * x, axis=-1, keepdims=True) + EPS) * g


def _inproj_kernel(x_ref, gmix_ref, w_ref, gcm_ref, q_ref, k_ref, v_ref, u_ref, vc_ref, qm_ref,
                   gate_ref, h_ref, *, q_scale):
    j = pl.program_id(1)

    @pl.when(j == 0)
    def _():
        h_ref[...] = _rmsnorm(x_ref[...], gmix_ref[...]).astype(BF16)

    z = jnp.dot(h_ref[...], w_ref[...], preferred_element_type=F32)

    @pl.when(j == 0)
    def _():
        q_ref[...] = (z * q_scale).astype(q_ref.dtype)

    @pl.when(j == 1)
    def _():
        k_ref[...] = z

    @pl.when(j == 2)
    def _():
        v_ref[...] = z

    @pl.when(j == 3)
    def _():
        u_ref[...] = jax.nn.gelu(z).astype(u_ref.dtype)

    @pl.when(j == 4)
    def _():
        vc_ref[...] = _rmsnorm(jax.nn.gelu(z), gcm_ref[...]).astype(vc_ref.dtype)

    @pl.when(j == 5)
    def _():
        qm_ref[...] = z.astype(qm_ref.dtype)

    @pl.when(j >= 6)
    def _():
        gate_ref[...] = jax.nn.sigmoid(z).astype(gate_ref.dtype)


def _inproj(x, g_mix, w_in, g_cm, *, width, q_scale, vc_dtype):
    t, d = x.shape
    n_tiles = w_in.shape[1] // width
    n_branch_tiles = 6
    tm = min(ROW_TILE, t)
    row = lambda i, j: (i, 0)
    outs = [
        jax.ShapeDtypeStruct((t, width), BF16),
        jax.ShapeDtypeStruct((t, width), F32),
        jax.ShapeDtypeStruct((t, width), F32),
        jax.ShapeDtypeStruct((t, width), BF16),
        jax.ShapeDtypeStruct((t, width), vc_dtype),
        jax.ShapeDtypeStruct((t, width), BF16),
        jax.ShapeDtypeStruct((t, (n_tiles - n_branch_tiles) * width), BF16),
    ]
    out_specs = [pl.BlockSpec((tm, width), row) for _ in range(6)]
    out_specs.append(pl.BlockSpec((tm, width), lambda i, j: (i, jnp.maximum(j - n_branch_tiles, 0))))
    return pl.pallas_call(
        functools.partial(_inproj_kernel, q_scale=q_scale),
        grid=(t // tm, n_tiles),
        in_specs=[
            pl.BlockSpec((tm, d), row),
            pl.BlockSpec((1, d), lambda i, j: (0, 0)),
            pl.BlockSpec((d, width), lambda i, j: (0, j)),
            pl.BlockSpec((1, width), lambda i, j: (0, 0)),
        ],
        out_specs=out_specs,
        out_shape=outs,
        scratch_shapes=[pltpu.VMEM((tm, d), BF16)],
        compiler_params=_params("parallel", "arbitrary"),
        name="inproj",
    )(x, g_mix.reshape(1, d), w_in, g_cm.reshape(1, width))


def _cumsum_matrix(kt):
    r = lax.broadcasted_iota(jnp.int32, (2 * kt, 2 * kt), 0)
    r = jnp.where(r >= kt, r - kt, r)
    c = lax.broadcasted_iota(jnp.int32, (2 * kt, 2 * kt), 1)
    return jnp.where((c >= kt) | (r > c), 1.0, 0.0).astype(BF16)


def _sb_tile(s, mask, rest, cmat):
    kt = s.shape[1]
    softplus_tail = jnp.log1p(jnp.exp(-jnp.abs(s)))
    log_beta = jnp.minimum(s, 0.0) - softplus_tail
    log_keep = -jnp.maximum(s, 0.0) - softplus_tail
    if mask is not None:
        log_keep = jnp.where(mask, log_keep, 0.0)
    hi = log_keep.astype(BF16)
    lo = (log_keep - hi.astype(F32)).astype(BF16)
    sums = jnp.dot(jnp.concatenate([hi, lo], axis=1), cmat, preferred_element_type=F32)
    a = jnp.exp(log_beta + sums[:, :kt] + rest)
    if mask is not None:
        a = jnp.where(mask, a, 0.0)
    return a, rest + sums[:, kt:]


def _qk(q, k):
    return lax.dot_general(q, k, (((1,), (1,)), ((), ())), preferred_element_type=F32)


def _sb_prompt_kernel(bias_ref, q_ref, k_ref, v_ref, o_ref, kb_ref, vb_ref, cmat_ref):
    bias = bias_ref[pl.program_id(1)]
    kb_ref[...] = k_ref[0].astype(BF16)
    vb_ref[...] = v_ref[0].astype(BF16)
    cmat_ref[...] = _cumsum_matrix(SB_TILE)
    n_q = q_ref.shape[1] // SB_TILE
    hd = q_ref.shape[2]
    row = lax.broadcasted_iota(jnp.int32, (SB_TILE, SB_TILE), 0)
    col = lax.broadcasted_iota(jnp.int32, (SB_TILE, SB_TILE), 1)
    causal = col < row

    def tile(q, kj, mask, acc, rest):
        ks = pl.ds(pl.multiple_of(kj * SB_TILE, SB_TILE), SB_TILE)
        s = _qk(q, kb_ref[ks, :]) + bias
        a, rest = _sb_tile(s, mask, rest, cmat_ref[...])
        acc = acc + jnp.dot(a.astype(BF16), vb_ref[ks, :], preferred_element_type=F32)
        return acc, rest

    def q_block(qi, carry):
        qs = pl.ds(pl.multiple_of(qi * SB_TILE, SB_TILE), SB_TILE)
        q = q_ref[0, qs, :]
        acc = jnp.zeros((SB_TILE, hd), F32)
        rest = jnp.zeros((SB_TILE, SB_TILE), F32)
        acc, rest = tile(q, qi, causal, acc, rest)

        def earlier(it, c):
            return tile(q, qi - 1 - it, None, *c)

        acc, rest = lax.fori_loop(0, qi, earlier, (acc, rest))
        o_ref[0, qs, :] = acc.astype(o_ref.dtype)
        return carry

    lax.fori_loop(0, n_q, q_block, 0)


def _sb_prompt(q, k, v, sb_bias, heads):
    b, s, width = q.shape
    hd = width // heads
    blk = lambda: pl.BlockSpec((1, s, hd), lambda bi, hi: (bi, 0, hi))
    return pl.pallas_call(
        _sb_prompt_kernel,
        grid=(b, heads),
        in_specs=[pl.BlockSpec(memory_space=pltpu.SMEM), blk(), blk(), blk()],
        out_specs=blk(),
        out_shape=jax.ShapeDtypeStruct((b, s, width), BF16),
        scratch_shapes=[pltpu.VMEM((s, hd), BF16), pltpu.VMEM((s, hd), BF16),
                        pltpu.VMEM((2 * SB_TILE, 2 * SB_TILE), BF16)],
        compiler_params=_params("parallel", "parallel"),
        name="sb_prompt",
    )(sb_bias, q, k, v)


def _sb_sample_kernel(pt_ref, bias_ref, q_ref, kn_ref, vn_ref, *rest_refs, n_pages, heads):
    del pt_ref
    k_pages = rest_refs[:n_pages]
    v_pages = rest_refs[n_pages:2 * n_pages]
    o_ref, new_ref = rest_refs[2 * n_pages:]
    n_tok, width = q_ref.shape[1], q_ref.shape[2]
    hd = width // heads
    rows = n_tok * heads

    rr = lax.broadcasted_iota(jnp.int32, (rows, width), 0)
    cc = lax.broadcasted_iota(jnp.int32, (rows, width), 1)
    own_head = (rr % heads) == (cc // hd)
    q = q_ref[0].astype(F32)
    q_rows = jnp.broadcast_to(q[:, None, :], (n_tok, heads, width)).reshape(rows, width)
    q_rows = jnp.where(own_head, q_rows, 0.0).astype(BF16)
    bias = bias_ref[...]
    cmat = _cumsum_matrix(SB_TILE)

    def tile(kp, vp, mask, acc, rest):
        s = _qk(q_rows, kp) + bias
        a, rest = _sb_tile(s, mask, rest, cmat)
        return acc + jnp.dot(a.astype(BF16), vp, preferred_element_type=F32), rest

    acc = jnp.zeros((rows, width), F32)
    rest = jnp.zeros((rows, SB_TILE), F32)

    tr = lax.broadcasted_iota(jnp.int32, (rows, SB_TILE), 0) // heads
    tc = lax.broadcasted_iota(jnp.int32, (rows, SB_TILE), 1)
    new_ref[...] = jnp.zeros_like(new_ref)
    new_ref[0, 0:n_tok, :] = kn_ref[0]
    new_ref[1, 0:n_tok, :] = vn_ref[0]
    acc, rest = tile(new_ref[0].astype(BF16), new_ref[1].astype(BF16), tc < tr, acc, rest)

    def page_rows(ref):
        return jnp.concatenate(
            [ref[0, pl.ds(h, SB_TILE, stride=heads), :] for h in range(heads)], axis=1).astype(BF16)

    for p in reversed(range(n_pages)):
        acc, rest = tile(page_rows(k_pages[p]), page_rows(v_pages[p]), None, acc, rest)

    acc = jnp.where(own_head, acc, 0.0).reshape(n_tok, heads, width)
    o_ref[0] = jnp.sum(acc, axis=1).astype(o_ref.dtype)


def _sb_sample(q, k_new, v_new, pool_k, pool_v, page_table, sb_bias, heads):
    nb, n_tok, width = q.shape
    n_pages = page_table.shape[1]
    hd = width // heads
    assert pool_k.shape[1:] == (SB_TILE * heads, hd) and n_tok <= V7X_SUBLANES
    rows = n_tok * heads
    bias_rows = jnp.broadcast_to(jnp.tile(sb_bias, n_tok)[:, None], (rows, SB_TILE))
    tok = lambda: pl.BlockSpec((1, n_tok, width), lambda b, pt: (b, 0, 0))
    page_spec = lambda p: pl.BlockSpec((1, SB_TILE * heads, hd), lambda b, pt: (pt[b, p], 0, 0))
    grid_spec = pltpu.PrefetchScalarGridSpec(
        num_scalar_prefetch=1,
        grid=(nb,),
        in_specs=[pl.BlockSpec((rows, SB_TILE), lambda b, pt: (0, 0)), tok(), tok(), tok()]
        + [page_spec(p) for p in range(n_pages)] * 2,
        out_specs=tok(),
        scratch_shapes=[pltpu.VMEM((2, SB_TILE, width), F32)],
    )
    return pl.pallas_call(
        functools.partial(_sb_sample_kernel, n_pages=n_pages, heads=heads),
        grid_spec=grid_spec,
        out_shape=jax.ShapeDtypeStruct((nb, n_tok, width), BF16),
        compiler_params=_params("parallel"),
        name="sb_sample",
    )(page_table, bias_rows, q, k_new, v_new, *([pool_k] * n_pages), *([pool_v] * n_pages))


def _cm_gate_kernel(u_ref, vc_ref, w_ref, b_ref, o_ref):
    groups, chunk, _ = w_ref.shape
    gd = u_ref.shape[1] // groups
    row = lax.broadcasted_iota(jnp.int32, (chunk, chunk), 0)
    col = lax.broadcasted_iota(jnp.int32, (chunk, chunk), 1)
    lower = col <= row
    for g in range(groups):
        w = jnp.where(lower, w_ref[g], 0.0).astype(BF16)
        bias = b_ref[g]
        cs = slice(g * gd, (g + 1) * gd)
        for c in range(u_ref.shape[0] // chunk):
            rs = slice(c * chunk, (c + 1) * chunk)
            mixed = jnp.dot(w, vc_ref[rs, cs].astype(BF16), preferred_element_type=F32) + bias
            o_ref[rs, cs] = (u_ref[rs, cs].astype(F32) * mixed).astype(o_ref.dtype)


def _cm_gate(u, vc, w_mix, b_rows):
    t, width = u.shape
    groups, chunk, _ = w_mix.shape
    tm = min(ROW_TILE, t)
    row = lambda i: (i, 0)
    const = lambda i: (0, 0, 0)
    return pl.pallas_call(
        _cm_gate_kernel,
        grid=(t // tm,),
        in_specs=[pl.BlockSpec((tm, width), row), pl.BlockSpec((tm, width), row),
                  pl.BlockSpec((groups, chunk, chunk), const),
                  pl.BlockSpec((groups, chunk, width // groups), const)],
        out_specs=pl.BlockSpec((tm, width), row),
        out_shape=jax.ShapeDtypeStruct((t, width), BF16),
        compiler_params=_params("parallel"),
        name="cm_gate",
    )(u, vc, w_mix, b_rows)


def _mem_kv_kernel(x_ref, g_ref, wk_ref, wv_ref, k_ref, v_ref, h_ref):
    @pl.when(pl.program_id(0) == 0)
    def _():
        h_ref[...] = _rmsnorm(x_ref[...], g_ref[...]).astype(BF16)

    k_ref[...] = jnp.dot(h_ref[...], wk_ref[...], preferred_element_type=F32)
    v_ref[...] = jnp.dot(h_ref[...], wv_ref[...], preferred_element_type=F32)


def _mem_kv(mem, g_mem, w_k, w_v):
    t, d = mem.shape
    width = w_k.shape[1]
    tn = 512
    col = lambda j: (0, j)
    const = lambda j: (0, 0)
    return pl.pallas_call(
        _mem_kv_kernel,
        grid=(width // tn,),
        in_specs=[pl.BlockSpec((t, d), const), pl.BlockSpec((1, d), const),
                  pl.BlockSpec((d, tn), col), pl.BlockSpec((d, tn), col)],
        out_specs=[pl.BlockSpec((t, tn), col), pl.BlockSpec((t, tn), col)],
        out_shape=[jax.ShapeDtypeStruct((t, width), F32)] * 2,
        scratch_shapes=[pltpu.VMEM((t, d), BF16)],
        compiler_params=_params("arbitrary"),
        name="mem_kv",
    )(mem, g_mem.reshape(1, d), w_k, w_v)


def _mem_attn_kernel(q_ref, k_ref, v_ref, o_ref, *, heads, scale):
    hd = q_ref.shape[2] // heads
    for bi in range(q_ref.shape[0]):
        for h in range(heads):
            cs = slice(h * hd, (h + 1) * hd)
            s = _qk(q_ref[bi, :, cs], k_ref[bi, :, cs].astype(BF16)) * scale
            p = jnp.exp(s - jnp.max(s, axis=-1, keepdims=True))
            o = jnp.dot(p.astype(BF16), v_ref[bi, :, cs].astype(BF16), preferred_element_type=F32)
            o_ref[bi, :, cs] = (o / jnp.sum(p, axis=-1, keepdims=True)).astype(o_ref.dtype)


def _mem_attn(q, k, v, heads, *, batch_block, q_tile):
    nb, t, width = q.shape
    m = k.shape[1]
    scale = 1.0 / math.sqrt(width // heads)
    return pl.pallas_call(
        functools.partial(_mem_attn_kernel, heads=heads, scale=scale),
        grid=(nb // batch_block, t // q_tile),
        in_specs=[pl.BlockSpec((batch_block, q_tile, width), lambda b, i: (b, i, 0)),
                  pl.BlockSpec((batch_block, m, width), lambda b, i: (b, 0, 0)),
                  pl.BlockSpec((batch_block, m, width), lambda b, i: (b, 0, 0))],
        out_specs=pl.BlockSpec((batch_block, q_tile, width), lambda b, i: (b, i, 0)),
        out_shape=jax.ShapeDtypeStruct((nb, t, width), BF16),
        compiler_params=_params("parallel", "parallel"),
        name="mem_attn",
    )(q, k, v)


def _merge_kernel(osb_ref, ocm_ref, omem_ref, wsb_ref, wcm_ref, wmem_ref, g0_ref, g1_ref, g2_ref,
                  o_ref):
    def branch(x_ref, w_ref, g_ref):
        return g_ref[...].astype(F32) * jnp.dot(x_ref[...], w_ref[...], preferred_element_type=F32)

    merged = (branch(osb_ref, wsb_ref, g0_ref) + branch(ocm_ref, wcm_ref, g1_ref)
              + branch(omem_ref, wmem_ref, g2_ref))
    o_ref[...] = merged.astype(o_ref.dtype)


def _merge(o_sb, o_cm, o_mem, w_sb_o, w_cm_o, w_mem_o, gates):
    t, width = o_sb.shape
    d = w_sb_o.shape[1]
    tm = min(ROW_TILE, t)
    tn = 512
    n_col = d // tn
    act = lambda: pl.BlockSpec((tm, width), lambda i, j: (i, 0))
    wgt = lambda: pl.BlockSpec((width, tn), lambda i, j: (0, j))
    gate = lambda br: pl.BlockSpec((tm, tn), lambda i, j: (i, br * n_col + j))
    return pl.pallas_call(
        _merge_kernel,
        grid=(t // tm, n_col),
        in_specs=[act(), act(), act(), wgt(), wgt(), wgt(), gate(0), gate(1), gate(2)],
        out_specs=pl.BlockSpec((tm, tn), lambda i, j: (i, j)),
        out_shape=jax.ShapeDtypeStruct((t, d), BF16),
        compiler_params=_params("parallel", "parallel"),
        name="merge",
    )(o_sb, o_cm, o_mem, w_sb_o, w_cm_o, w_mem_o, gates, gates, gates)


def _outproj_kernel(x_ref, m_ref, w_ref, o_ref):
    o_ref[...] = x_ref[...] + jnp.dot(m_ref[...], w_ref[...], preferred_element_type=F32)


def _outproj(x, merged, w_out):
    t, d = x.shape
    tm = min(ROW_TILE, t)
    tn = 1024
    return pl.pallas_call(
        _outproj_kernel,
        grid=(t // tm, d // tn),
        in_specs=[pl.BlockSpec((tm, tn), lambda i, j: (i, j)),
                  pl.BlockSpec((tm, d), lambda i, j: (i, 0)),
                  pl.BlockSpec((d, tn), lambda i, j: (0, j))],
        out_specs=pl.BlockSpec((tm, tn), lambda i, j: (i, j)),
        out_shape=jax.ShapeDtypeStruct((t, d), F32),
        compiler_params=_params("parallel", "parallel"),
        name="outproj",
    )(x, merged, w_out)


def _ffn_kernel(x_ref, gffn_ref, wg_ref, wu_ref, wd_ref, gfin_ref, o_ref, h_ref, acc_ref):
    f = pl.program_id(1)

    @pl.when(f == 0)
    def _():
        h_ref[...] = _rmsnorm(x_ref[...], gffn_ref[...]).astype(BF16)
        acc_ref[...] = jnp.zeros_like(acc_ref)

    h = h_ref[...]
    gate = jnp.dot(h, wg_ref[...], preferred_element_type=F32)
    up = jnp.dot(h, wu_ref[...], preferred_element_type=F32)
    act = (jax.nn.silu(gate) * up).astype(BF16)
    acc_ref[...] += jnp.dot(act, wd_ref[...], preferred_element_type=F32)

    @pl.when(f == pl.num_programs(1) - 1)
    def _():
        o_ref[...] = _rmsnorm(x_ref[...] + acc_ref[...], gfin_ref[...])


def _ffn(x, g_ffn, w_gate, w_up, w_down, g_final):
    t, d = x.shape
    d_ff = w_gate.shape[1]
    tm = min(ROW_TILE, t)
    tf = 512
    row = lambda i, f: (i, 0)
    const = lambda i, f: (0, 0)
    return pl.pallas_call(
        _ffn_kernel,
        grid=(t // tm, d_ff // tf),
        in_specs=[pl.BlockSpec((tm, d), row), pl.BlockSpec((1, d), const),
                  pl.BlockSpec((d, tf), lambda i, f: (0, f)),
                  pl.BlockSpec((d, tf), lambda i, f: (0, f)),
                  pl.BlockSpec((tf, d), lambda i, f: (f, 0)),
                  pl.BlockSpec((1, d), const)],
        out_specs=pl.BlockSpec((tm, d), row),
        out_shape=jax.ShapeDtypeStruct((t, d), F32),
        scratch_shapes=[pltpu.VMEM((tm, d), BF16), pltpu.VMEM((tm, d), F32)],
        compiler_params=_params("parallel", "arbitrary"),
        name="ffn",
    )(x, g_ffn.reshape(1, d), w_gate, w_up, w_down, g_final.reshape(1, d))


def kernel(x_prompt, x_sample, cache_sb_k, cache_sb_v, cache_mem_k, cache_mem_v, page_table, mem_prompt, g_mix, w_in, sb_bias, w_s, b_s, g_cm, g_mem, w_mem_k, w_mem_v, w_sb_o, w_cm_o, w_mem_o, w_out, g_ffn, w_gate, w_up, w_down, g_final):
    depth = w_in.shape[0]
    assert depth == 1, "single trunk layer"
    batch, seq, d = x_prompt.shape
    dec_batch, dec_seq, _ = x_sample.shape
    _, n_phys, page, sb_heads, sb_hd = cache_sb_k.shape
    _, _, mem_len, mem_heads, mem_hd = cache_mem_k.shape
    groups, chunk, _ = w_s.shape[1:]
    width = sb_heads * sb_hd
    assert width == mem_heads * mem_hd == g_cm.shape[1] and d % width == 0
    assert w_in.shape[2] == 6 * width + 3 * d
    assert seq % chunk == 0 and dec_seq <= chunk and chunk % dec_seq == 0
    gd = width // groups

    bf = lambda w: w[0].astype(BF16)
    w_in_b, w_sb_o_b, w_cm_o_b, w_mem_o_b = bf(w_in), bf(w_sb_o), bf(w_cm_o), bf(w_mem_o)
    w_out_b, w_gate_b, w_up_b, w_down_b = bf(w_out), bf(w_gate), bf(w_up), bf(w_down)
    q_scale = 1.0 / math.sqrt(sb_hd)

    w_mix_p = w_s[0]
    b_rows_p = jnp.broadcast_to(b_s[0][:, :, None], (groups, chunk, gd))
    n_rep = chunk // dec_seq
    eye = jnp.eye(n_rep, dtype=F32)
    w_mix_s = jnp.einsum("ab,gts->gatbs", eye, w_s[0][:, :dec_seq, :dec_seq]).reshape(groups, chunk, chunk)
    b_rows_s = jnp.broadcast_to(jnp.tile(b_s[0][:, :dec_seq], (1, n_rep))[:, :, None], (groups, chunk, gd))

    def dense_tail(x, o_sb, o_cm, o_mem, gates):
        merged = _merge(o_sb, o_cm, o_mem, w_sb_o_b, w_cm_o_b, w_mem_o_b, gates)
        x1 = _outproj(x, merged, w_out_b)
        return _ffn(x1, g_ffn[0], w_gate_b, w_up_b, w_down_b, g_final)

    xp = x_prompt.reshape(batch * seq, d)
    q, k, v, u, vc, qm, gates = _inproj(xp, g_mix[0], w_in_b, g_cm[0], width=width, q_scale=q_scale,
                                        vc_dtype=BF16)
    mem_k, mem_v = _mem_kv(mem_prompt.reshape(batch * mem_len, d), g_mem[0], bf(w_mem_k), bf(w_mem_v))
    o_sb = _sb_prompt(q.reshape(batch, seq, width), k.reshape(batch, seq, width),
                      v.reshape(batch, seq, width), sb_bias[0], sb_heads)
    o_cm = _cm_gate(u, vc, w_mix_p, b_rows_p)
    o_mem = _mem_attn(qm.reshape(batch, seq, width), mem_k.reshape(batch, mem_len, width),
                      mem_v.reshape(batch, mem_len, width), mem_heads, batch_block=1, q_tile=ROW_TILE)
    y_prompt = dense_tail(xp, o_sb.reshape(batch * seq, width), o_cm,
                          o_mem.reshape(batch * seq, width), gates)

    xs = x_sample.reshape(dec_batch * dec_seq, d)
    qs, ks, vs, us, vcs, qms, gates_s = _inproj(xs, g_mix[0], w_in_b, g_cm[0], width=width,
                                                q_scale=q_scale, vc_dtype=F32)
    o_sb_s = _sb_sample(qs.reshape(dec_batch, dec_seq, width), ks.reshape(dec_batch, dec_seq, width),
                        vs.reshape(dec_batch, dec_seq, width),
                        cache_sb_k[0].reshape(n_phys, page * sb_heads, sb_hd),
                        cache_sb_v[0].reshape(n_phys, page * sb_heads, sb_hd), page_table,
                        sb_bias[0], sb_heads)
    o_cm_s = _cm_gate(us, vcs, w_mix_s, b_rows_s)
    o_mem_s = _mem_attn(qms.reshape(dec_batch, dec_seq, width),
                        cache_mem_k[0].reshape(dec_batch, mem_len, width),
                        cache_mem_v[0].reshape(dec_batch, mem_len, width), mem_heads,
                        batch_block=4, q_tile=dec_seq)
    y_sample = dense_tail(xs, o_sb_s.reshape(dec_batch * dec_seq, width), o_cm_s,
                          o_mem_s.reshape(dec_batch * dec_seq, width), gates_s)

    return (
        y_prompt.reshape(batch, seq, d),
        y_sample.reshape(dec_batch, dec_seq, d),
        k.reshape(1, batch, seq, sb_heads, sb_hd),
        v.reshape(1, batch, seq, sb_heads, sb_hd),
        mem_k.reshape(1, batch, mem_len, mem_heads, mem_hd),
        mem_v.reshape(1, batch, mem_len, mem_heads, mem_hd),
        ks.reshape(1, dec_batch, dec_seq, sb_heads, sb_hd),
        vs.reshape(1, dec_batch, dec_seq, sb_heads, sb_hd),
        vcs.reshape(1, dec_batch, dec_seq, groups, gd),
    )
```
